```python
import math
import jax, jax.numpy as jnp
from jax import lax
import numpy as np

D_MODEL = 4096
BATCH = 2
SEQ = 4096
DEPTH = 2

N_A_LAYERS = DEPTH // 2
N_B_LAYERS = DEPTH - N_A_LAYERS
EPS = 1e-6

GDN_HEAD_DIM = 128
GDN_HEADS = D_MODEL // GDN_HEAD_DIM
GDN_WIDTH = GDN_HEADS * GDN_HEAD_DIM
GDN_IN_COLS = 4 * GDN_WIDTH + 2 * GDN_HEADS
CONV_WIDTH = 4
CHUNK = 64

DIFF_HEAD_DIM = 128
DIFF_HEADS = D_MODEL // (2 * DIFF_HEAD_DIM)
DIFF_V_DIM = 2 * DIFF_HEAD_DIM
DIFF_QK_WIDTH = DIFF_HEADS * 2 * DIFF_HEAD_DIM
DIFF_V_WIDTH = DIFF_HEADS * DIFF_V_DIM
Q_BLOCK = 128

PEER_HEADS = 8
PEER_KEYS = 128
PEER_EXPERTS = PEER_KEYS * PEER_KEYS
PEER_HALF = 128
PEER_QUERY_DIM = 2 * PEER_HALF
PEER_TOPK = 16
PEER_TOKEN_BLOCK = 64

kernel_name = "yoco_gdn_diffattn_peer"


def rms_norm(x, gain):
    xf = x.astype(jnp.float32)
    y = xf * lax.rsqrt(jnp.mean(xf * xf, axis=-1, keepdims=True) + EPS)
    return (y * gain.astype(jnp.float32)).astype(x.dtype)


def l2_norm(x):
    xf = x.astype(jnp.float32)
    return xf * lax.rsqrt(jnp.sum(xf * xf, axis=-1, keepdims=True) + EPS)


def causal_depthwise_conv(x, w):
    s = x.shape[1]
    xp = jnp.pad(x, ((0, 0), (CONV_WIDTH - 1, 0), (0, 0)))
    return sum(xp[:, j:j + s] * w[j] for j in range(CONV_WIDTH))


def gated_delta_rule(q, k, v, g, beta):
    b, s, h, dk = q.shape
    dv = v.shape[-1]
    n = s // CHUNK

    def to_chunks(t):
        t = t.reshape(b, n, CHUNK, h, *t.shape[3:])
        return jnp.moveaxis(t, (1, 3), (0, 2))

    q = to_chunks(q) * (dk ** -0.5)
    k, v = to_chunks(k), to_chunks(v)
    g_cum = jnp.cumsum(to_chunks(g), axis=-1)
    beta = to_chunks(beta)
    incl = jnp.tril(jnp.ones((CHUNK, CHUNK), dtype=bool))
    strict = jnp.tril(jnp.ones((CHUNK, CHUNK), dtype=bool), -1)
    decay = jnp.exp(jnp.where(incl, g_cum[..., :, None] - g_cum[..., None, :], -jnp.inf))
    k_beta = k * beta[..., None]
    a_mat = jnp.where(strict, jnp.einsum('nbhcd,nbhed->nbhce', k_beta, k) * decay, 0.0)
    eye = jnp.eye(CHUNK, dtype=jnp.float32)
    t_mat = lax.linalg.triangular_solve(eye + a_mat, jnp.broadcast_to(eye, a_mat.shape),
                                        left_side=True, lower=True)
    u = jnp.einsum('nbhce,nbhef->nbhcf', t_mat, v * beta[..., None])
    w = jnp.einsum('nbhce,nbhed->nbhcd', t_mat, k_beta * jnp.exp(g_cum)[..., None])
    qk = jnp.where(incl, jnp.einsum('nbhcd,nbhed->nbhce', q, k) * decay, 0.0)

    def step(state, xs):
        q_c, k_c, u_c, w_c, g_c, qk_c = xs
        v_new = u_c - jnp.einsum('bhcd,bhde->bhce', w_c, state)
        o = (jnp.einsum('bhcd,bhde->bhce', q_c * jnp.exp(g_c)[..., None], state)
             + jnp.einsum('bhce,bhef->bhcf', qk_c, v_new))
        g_last = g_c[..., -1]
        k_dec = k_c * jnp.exp(g_last[..., None] - g_c)[..., None]
        state = state * jnp.exp(g_last)[..., None, None] + jnp.einsum('bhcd,bhce->bhde', k_dec, v_new)
        return state, o

    state0 = jnp.zeros((b, h, dk, dv), jnp.float32)
    _, o = lax.scan(step, state0, (q, k, u, w, g_cum, qk))
    return jnp.moveaxis(o, (0, 2), (1, 3)).reshape(b, s, h, dv)


def gated_deltanet_mixer(h, w_in, conv_w, a_log, dt_bias, o_gain, w_out):
    b, s, _ = h.shape
    proj = h @ w_in
    qkv = proj[..., :3 * GDN_WIDTH]
    z = proj[..., 3 * GDN_WIDTH:4 * GDN_WIDTH]
    beta_raw = proj[..., 4 * GDN_WIDTH:4 * GDN_WIDTH + GDN_HEADS]
    a_raw = proj[..., 4 * GDN_WIDTH + GDN_HEADS:]
    qkv = jax.nn.silu(causal_depthwise_conv(qkv, conv_w))
    q, k, v = jnp.split(qkv, 3, axis=-1)
    shp = (b, s, GDN_HEADS, GDN_HEAD_DIM)
    q, k = l2_norm(q.reshape(shp)), l2_norm(k.reshape(shp))
    v = v.reshape(shp).astype(jnp.float32)
    beta = jax.nn.sigmoid(beta_raw.astype(jnp.float32))
    g = -jnp.exp(a_log.astype(jnp.float32)) * jax.nn.softplus(
        a_raw.astype(jnp.float32) + dt_bias.astype(jnp.float32))
    o = gated_delta_rule(q, k, v, g, beta)
    o = rms_norm(o, o_gain) * jax.nn.silu(z.reshape(shp).astype(jnp.float32))
    return o.reshape(b, s, GDN_WIDTH).astype(h.dtype) @ w_out


def shared_kv(h_stream, kv_gain, w_kv, k_gain):
    b, s, _ = h_stream.shape
    kv = rms_norm(h_stream, kv_gain) @ w_kv
    k = rms_norm(kv[..., :DIFF_QK_WIDTH].reshape(b, s, DIFF_HEADS, 2, DIFF_HEAD_DIM), k_gain)
    v = kv[..., DIFF_QK_WIDTH:].reshape(b, s, DIFF_HEADS, DIFF_V_DIM)
    return k.astype(jnp.float32), v.astype(jnp.float32)


def diff_attention_mixer(h, k, v, w_q, q_gain, lam_q1, lam_k1, lam_q2, lam_k2, sub_gain, w_o, lambda_init):
    b, s, _ = h.shape
    q = rms_norm((h @ w_q).reshape(b, s, DIFF_HEADS, 2, DIFF_HEAD_DIM), q_gain).astype(jnp.float32)
    lam = (jnp.exp(jnp.sum(lam_q1.astype(jnp.float32) * lam_k1.astype(jnp.float32)))
           - jnp.exp(jnp.sum(lam_q2.astype(jnp.float32) * lam_k2.astype(jnp.float32))) + lambda_init)
    scale = DIFF_HEAD_DIM ** -0.5
    nb = s // Q_BLOCK
    qb = q.reshape(b, nb, Q_BLOCK, DIFF_HEADS, 2, DIFF_HEAD_DIM).swapaxes(0, 1)
    key_pos = jnp.arange(s)

    def block(args):
        q_blk, i = args
        scores = jnp.einsum('bqhcd,bkhcd->bhcqk', q_blk, k) * scale
        q_pos = i * Q_BLOCK + jnp.arange(Q_BLOCK)
        mask = key_pos[None, :] <= q_pos[:, None]
        p = jax.nn.softmax(jnp.where(mask, scores, -jnp.inf), axis=-1)
        attn = p[:, :, 0] - lam * p[:, :, 1]
        return jnp.einsum('bhqk,bkhe->bqhe', attn, v)

    o = lax.map(block, (qb, jnp.arange(nb)))
    o = o.swapaxes(0, 1).reshape(b, s, DIFF_HEADS, DIFF_V_DIM)
    o = rms_norm(o, sub_gain) * (1.0 - lambda_init)
    return o.reshape(b, s, DIFF_V_WIDTH).astype(h.dtype) @ w_o


def peer_channel_mixer(h, w_query, sub_keys1, sub_keys2, expert_u, expert_v):
    b, s, dm = h.shape
    t = b * s
    x = h.reshape(t, dm)
    qry = (x @ w_query).reshape(t, PEER_HEADS, 2, PEER_HALF).astype(jnp.float32)
    s1 = jnp.einsum('thd,hnd->thn', qry[:, :, 0], sub_keys1.astype(jnp.float32))
    s2 = jnp.einsum('thd,hnd->thn', qry[:, :, 1], sub_keys2.astype(jnp.float32))
    v1, i1 = lax.top_k(s1, PEER_TOPK)
    v2, i2 = lax.top_k(s2, PEER_TOPK)
    cand = (v1[..., :, None] + v2[..., None, :]).reshape(t, PEER_HEADS, PEER_TOPK * PEER_TOPK)
    cand_idx = (i1[..., :, None] * PEER_KEYS + i2[..., None, :]).reshape(t, PEER_HEADS, PEER_TOPK * PEER_TOPK)
    top_s, pos = lax.top_k(cand, PEER_TOPK)
    experts = jnp.take_along_axis(cand_idx, pos, axis=-1)
    gates = jax.nn.softmax(top_s, axis=-1)
    nblk = t // PEER_TOKEN_BLOCK
    hk = PEER_HEADS * PEER_TOPK

    def block(args):
        x_b, e_b, g_b = args
        act = jax.nn.gelu(jnp.einsum('td,tkd->tk', x_b, expert_u[e_b]).astype(jnp.float32),
                          approximate=False)
        return jnp.einsum('tk,tkd->td', (g_b * act).astype(x_b.dtype), expert_v[e_b])

    y = lax.map(block, (x.reshape(nblk, PEER_TOKEN_BLOCK, dm),
                        experts.reshape(nblk, PEER_TOKEN_BLOCK, hk),
                        gates.reshape(nblk, PEER_TOKEN_BLOCK, hk)))
    return y.reshape(b, s, dm)


def setup_inputs(seed: int = 0) -> dict:
    key = jax.random.key(seed)
    ks = jax.random.split(key, 32)
    f32 = jnp.float32

    def nrm(k, shape, scale):
        return jax.random.normal(k, shape, f32) * scale

    def gain(k, shape):
        return 1.0 + 0.01 * jax.random.normal(k, shape, f32)

    dsc = D_MODEL ** -0.5
    dt = jnp.exp(jax.random.uniform(ks[5], (N_A_LAYERS, GDN_HEADS), f32,
                                    minval=math.log(1e-3), maxval=math.log(1e-1)))
    return {
        "x": nrm(ks[0], (BATCH, SEQ, D_MODEL), 1.0),
        "a_norm": gain(ks[1], (N_A_LAYERS, D_MODEL)),
        "a_w_in": nrm(ks[2], (N_A_LAYERS, D_MODEL, GDN_IN_COLS), dsc),
        "a_conv": nrm(ks[3], (N_A_LAYERS, CONV_WIDTH, 3 * GDN_WIDTH), CONV_WIDTH ** -0.5),
        "a_log": jnp.log(jax.random.uniform(ks[4], (N_A_LAYERS, GDN_HEADS), f32, minval=1.0, maxval=16.0)),
        "a_dt_bias": dt + jnp.log(-jnp.expm1(-dt)),
        "a_out_norm": gain(ks[6], (N_A_LAYERS, GDN_HEAD_DIM)),
        "a_w_out": nrm(ks[7], (N_A_LAYERS, GDN_WIDTH, D_MODEL), GDN_WIDTH ** -0.5),
        "kv_norm": gain(ks[8], (D_MODEL,)),
        "kv_w": nrm(ks[9], (D_MODEL, DIFF_QK_WIDTH + DIFF_V_WIDTH), dsc),
        "kv_k_norm": gain(ks[10], (DIFF_HEAD_DIM,)),
        "b_norm": gain(ks[11], (N_B_LAYERS, D_MODEL)),
        "b_w_q": nrm(ks[12], (N_B_LAYERS, D_MODEL, DIFF_QK_WIDTH), dsc),
        "b_q_norm": gain(ks[13], (N_B_LAYERS, DIFF_HEAD_DIM)),
        "b_lam_q1": nrm(ks[14], (N_B_LAYERS, DIFF_HEAD_DIM), 0.1),
        "b_lam_k1": nrm(ks[15], (N_B_LAYERS, DIFF_HEAD_DIM), 0.1),
        "b_lam_q2": nrm(ks[16], (N_B_LAYERS, DIFF_HEAD_DIM), 0.1),
        "b_lam_k2": nrm(ks[17], (N_B_LAYERS, DIFF_HEAD_DIM), 0.1),
        "b_sub_norm": gain(ks[18], (N_B_LAYERS, DIFF_V_DIM)),
        "b_w_o": nrm(ks[19], (N_B_LAYERS, DIFF_V_WIDTH, D_MODEL), DIFF_V_WIDTH ** -0.5),
        "f_norm": gain(ks[20], (DEPTH, D_MODEL)),
        "f_w_query": nrm(ks[21], (DEPTH, D_MODEL, PEER_HEADS * PEER_QUERY_DIM), dsc),
        "f_sub_keys1": nrm(ks[22], (DEPTH, PEER_HEADS, PEER_KEYS, PEER_HALF), PEER_HALF ** -0.5),
        "f_sub_keys2": nrm(ks[23], (DEPTH, PEER_HEADS, PEER_KEYS, PEER_HALF), PEER_HALF ** -0.5),
        "f_expert_u": nrm(ks[24], (DEPTH, PEER_EXPERTS, D_MODEL), dsc),
        "f_expert_v": nrm(ks[25], (DEPTH, PEER_EXPERTS, D_MODEL), PEER_HEADS ** -0.5),
    }


def reference(x, a_norm, a_w_in, a_conv, a_log, a_dt_bias, a_out_norm, a_w_out,
              kv_norm, kv_w, kv_k_norm,
              b_norm, b_w_q, b_q_norm, b_lam_q1, b_lam_k1, b_lam_q2, b_lam_k2, b_sub_norm, b_w_o,
              f_norm, f_w_query, f_sub_keys1, f_sub_keys2, f_expert_u, f_expert_v):
    k_shared = v_shared = None
    for i in range(DEPTH):
        if i < N_A_LAYERS:
            x = x + gated_deltanet_mixer(rms_norm(x, a_norm[i]), a_w_in[i], a_conv[i], a_log[i],
                                         a_dt_bias[i], a_out_norm[i], a_w_out[i])
        else:
            j = i - N_A_LAYERS
            if j == 0:
                k_shared, v_shared = shared_kv(x, kv_norm, kv_w, kv_k_norm)
            lambda_init = 0.8 - 0.6 * math.exp(-0.3 * i)
            x = x + diff_attention_mixer(rms_norm(x, b_norm[j]), k_shared, v_shared, b_w_q[j], b_q_norm[j],
                                         b_lam_q1[j], b_lam_k1[j], b_lam_q2[j], b_lam_k2[j],
                                         b_sub_norm[j], b_w_o[j], lambda_init)
        x = x + peer_channel_mixer(rms_norm(x, f_norm[i]), f_w_query[i], f_sub_keys1[i], f_sub_keys2[i],
                                   f_expert_u[i], f_expert_v[i])
    return x
```

```python
import functools
import math

import jax
import jax.numpy as jnp
from jax import lax
from jax.experimental import pallas as pl
from jax.experimental.pallas import tpu as pltpu

F32 = jnp.float32
BF16 = jnp.bfloat16

EPS = 1e-6
LANES = 128
VMEM_LIMIT = 56 * 1024 * 1024

GDN_HEAD_DIM = 128
CONV_WIDTH = 4
CHUNK = 64
DIFF_HEAD_DIM = 128
PEER_HEADS = 8
PEER_KEYS = 128
PEER_HALF = 128
PEER_TOPK = 16


def _params(*sem):
    return pltpu.CompilerParams(dimension_semantics=sem, vmem_limit_bytes=VMEM_LIMIT)


def _add_rmsnorm_kernel(*refs, n_in, n_gain, emit_sum):
    in_refs = refs[:n_in]
    gain_refs = refs[n_in:n_in + n_gain]
    out_refs = refs[n_in + n_gain:]
    x = in_refs[0][...]
    for r in in_refs[1:]:
        x = x + r[...]
    k = 0
    if emit_sum:
        out_refs[0][...] = x
        k = 1
    y = x * lax.rsqrt(jnp.mean(x * x, axis=-1, keepdims=True) + EPS)
    for g, o in zip(gain_refs, out_refs[k:]):
        o[...] = (y * g[...]).astype(o.dtype)


def add_rmsnorm(xs, gains, *, rows=128):
    t, d = xs[0].shape
    emit_sum = len(xs) > 1
    row_spec = pl.BlockSpec((rows, d), lambda i: (i, 0))
    gain_spec = pl.BlockSpec((1, d), lambda i: (0, 0))
    out_shape = ([jax.ShapeDtypeStruct((t, d), F32)] if emit_sum else []) + [
        jax.ShapeDtypeStruct((t, d), BF16) for _ in gains]
    outs = pl.pallas_call(
        functools.partial(_add_rmsnorm_kernel, n_in=len(xs), n_gain=len(gains), emit_sum=emit_sum),
        out_shape=out_shape,
        grid=(t // rows,),
        in_specs=[row_spec] * len(xs) + [gain_spec] * len(gains),
        out_specs=[row_spec] * len(out_shape),
        compiler_params=_params("parallel"),
        name="add_rmsnorm",
    )(*xs, *[g.reshape(1, d).astype(F32) for g in gains])
    return outs


def _add_kernel(a_ref, b_ref, o_ref):
    o_ref[...] = a_ref[...] + b_ref[...]


def add(a, b, *, rows=256):
    t, d = a.shape
    spec = pl.BlockSpec((rows, d), lambda i: (i, 0))
    return pl.pallas_call(
        _add_kernel, out_shape=jax.ShapeDtypeStruct((t, d), a.dtype), grid=(t // rows,),
        in_specs=[spec, spec], out_specs=spec, compiler_params=_params("parallel"), name="residual_add",
    )(a, b)


def _matmul_kernel(*refs, has_res, has_gain, tn):
    a_ref, b_ref = refs[0], refs[1]
    k = 2
    res_ref = gain_ref = None
    if has_res:
        res_ref = refs[k]
        k += 1
    if has_gain:
        gain_ref = refs[k]
        k += 1
    o_ref = refs[k]
    acc = jnp.dot(a_ref[...], b_ref[...], preferred_element_type=F32)
    if has_res:
        acc = acc + res_ref[...]
    if has_gain:
        for g in range(tn // LANES):
            blk = acc[:, g * LANES:(g + 1) * LANES]
            blk = blk * lax.rsqrt(jnp.mean(blk * blk, axis=-1, keepdims=True) + EPS)
            o_ref[:, g * LANES:(g + 1) * LANES] = (blk * gain_ref[...]).astype(o_ref.dtype)
    else:
        o_ref[...] = acc.astype(o_ref.dtype)


def matmul(a, b, *, out_dtype, residual=None, group_gain=None, tm=1024, tn=512):
    m, kd = a.shape
    n = b.shape[1]
    tm = min(tm, m)
    tn = min(tn, n)
    in_specs = [pl.BlockSpec((tm, kd), lambda i, j: (i, 0)),
                pl.BlockSpec((kd, tn), lambda i, j: (0, j))]
    args = [a, b]
    if residual is not None:
        in_specs.append(pl.BlockSpec((tm, tn), lambda i, j: (i, j)))
        args.append(residual)
    if group_gain is not None:
        in_specs.append(pl.BlockSpec((1, LANES), lambda i, j: (0, 0)))
        args.append(group_gain.reshape(1, LANES).astype(F32))
    return pl.pallas_call(
        functools.partial(_matmul_kernel, has_res=residual is not None,
                          has_gain=group_gain is not None, tn=tn),
        out_shape=jax.ShapeDtypeStruct((m, n), out_dtype),
        grid=(m // tm, n // tn),
        in_specs=in_specs,
        out_specs=pl.BlockSpec((tm, tn), lambda i, j: (i, j)),
        compiler_params=_params("parallel", "parallel"),
        name="matmul",
    )(*args)


def _gdn_prep_kernel(cur_ref, prev_ref, w_ref, o_ref, *, ts, n_qk_tiles, n_q_tiles, q_scale):
    i = pl.program_id(1)
    ct = pl.program_id(2)
    cur = cur_ref[0]
    prev = jnp.where(i > 0, prev_ref[0], 0.0)
    xcat = jnp.concatenate([prev, cur], axis=0)
    n = ts + 8
    w = w_ref[...]
    y = cur * w[CONV_WIDTH - 1:CONV_WIDTH, :]
    for j in range(CONV_WIDTH - 1):
        d = CONV_WIDTH - 1 - j
        y = y + pltpu.roll(xcat, d, 0)[8:n, :] * w[j:j + 1, :]
    y = y * jax.nn.sigmoid(y)
    is_qk = ct < n_qk_tiles
    scale = jnp.where(ct < n_q_tiles, q_scale, 1.0)
    tc = y.shape[1]
    for g in range(tc // LANES):
        blk = y[:, g * LANES:(g + 1) * LANES]
        rs = lax.rsqrt(jnp.sum(blk * blk, axis=-1, keepdims=True) + EPS) * scale
        o_ref[0, :, g * LANES:(g + 1) * LANES] = (blk * jnp.where(is_qk, rs, 1.0)).astype(o_ref.dtype)


def gdn_prep(proj, conv_w, *, width, ts=256, tc=512):
    b, s, _ = proj.shape
    ncol = 3 * width
    cur_spec = pl.BlockSpec((1, ts, tc), lambda bi, i, c: (bi, i, c))
    prev_spec = pl.BlockSpec((1, 8, tc), lambda bi, i, c: (bi, jnp.maximum(i * (ts // 8) - 1, 0), c))
    return pl.pallas_call(
        functools.partial(_gdn_prep_kernel, ts=ts, n_qk_tiles=2 * width // tc, n_q_tiles=width // tc,
                          q_scale=GDN_HEAD_DIM ** -0.5),
        out_shape=jax.ShapeDtypeStruct((b, s, ncol), BF16),
        grid=(b, s // ts, ncol // tc),
        in_specs=[cur_spec, prev_spec, pl.BlockSpec((CONV_WIDTH, tc), lambda bi, i, c: (0, c))],
        out_specs=cur_spec,
        compiler_params=_params("parallel", "parallel", "parallel"),
        name="gdn_prep",
    )(proj, proj, conv_w)


def _gdn_gates_kernel(bg_ref, alog_ref, dtb_ref, beta_ref, gc_ref, *, heads):
    bg = bg_ref[0]
    beta_ref[0] = jax.nn.sigmoid(bg[:heads])
    g = -jnp.exp(alog_ref[...]) * jax.nn.softplus(bg[heads:] + dtb_ref[...])
    pos = lax.broadcasted_iota(jnp.int32, g.shape, 1) % CHUNK
    step = 1
    while step < CHUNK:
        g = g + jnp.where(pos >= step, pltpu.roll(g, step, 1), 0.0)
        step *= 2
    gc_ref[0] = g


def gdn_gates(bg_t, a_log, dt_bias, *, ts=512):
    b, h2, s = bg_t.shape
    heads = h2 // 2
    ts = min(ts, s)
    out_spec = pl.BlockSpec((1, heads, ts), lambda bi, i: (bi, 0, i))
    col_spec = pl.BlockSpec((heads, 1), lambda bi, i: (0, 0))
    return pl.pallas_call(
        functools.partial(_gdn_gates_kernel, heads=heads),
        out_shape=[jax.ShapeDtypeStruct((b, heads, s), F32)] * 2,
        grid=(b, s // ts),
        in_specs=[pl.BlockSpec((1, h2, ts), lambda bi, i: (bi, 0, i)), col_spec, col_spec],
        out_specs=[out_spec, out_spec],
        compiler_params=_params("parallel", "parallel"),
        name="gdn_gates",
    )(bg_t, a_log.reshape(heads, 1).astype(F32), dt_bias.reshape(heads, 1).astype(F32))


def _dot_nt(a, b):
    return lax.dot_general(a, b, (((1,), (1,)), ((), ())), preferred_element_type=F32)


def _dot_tn(a, b):
    return lax.dot_general(a, b, (((0,), (0,)), ((), ())), preferred_element_type=F32)


def _gdn_core_kernel(q_ref, k_ref, v_ref, z_ref, gc_ref, beta_ref, gain_ref, o_ref, state_ref, *, n_chunks):
    c = CHUNK
    row = lax.broadcasted_iota(jnp.int32, (c, c), 0)
    col = lax.broadcasted_iota(jnp.int32, (c, c), 1)
    eye = row == col
    incl = row >= col
    strict = row > col
    eye_f = eye.astype(F32)
    state_ref[...] = jnp.zeros_like(state_ref)

    def to_col(r):
        return jnp.sum(jnp.where(eye, jnp.broadcast_to(r, (c, c)), 0.0), axis=1, keepdims=True)

    def body(n, carry):
        t0 = pl.multiple_of(n * c, c)
        q = q_ref[0, pl.ds(t0, c), :]
        k = k_ref[0, pl.ds(t0, c), :]
        v = v_ref[0, pl.ds(t0, c), :]
        g_row = gc_ref[0, 0, pl.ds(n, 1), :]
        b_row = beta_ref[0, 0, pl.ds(n, 1), :]
        g_col = to_col(g_row)
        b_col = to_col(b_row)
        decay = jnp.exp(jnp.where(incl, g_col - g_row, -jnp.inf))
        kk = _dot_nt(k, k)
        a_neg = jnp.where(strict, -(kk * b_col * decay), 0.0)
        t_mat = eye_f + a_neg
        p = a_neg
        for _ in range(5):
            pb = p.astype(BF16)
            p = jnp.dot(pb, pb, preferred_element_type=F32)
            t_mat = t_mat + jnp.dot(t_mat.astype(BF16), p.astype(BF16), preferred_element_type=F32)
        u = jnp.dot((t_mat * b_row).astype(BF16), v, preferred_element_type=F32)
        w = jnp.dot((t_mat * (b_row * jnp.exp(g_row))).astype(BF16), k, preferred_element_type=F32)
        qk = jnp.where(incl, _dot_nt(q, k) * decay, 0.0)
        state = state_ref[...]
        state_b = state.astype(BF16)
        v_new = u - jnp.dot(w.astype(BF16), state_b, preferred_element_type=F32)
        o = (jnp.dot(q, state_b, preferred_element_type=F32) * jnp.exp(g_col)
             + jnp.dot(qk.astype(BF16), v_new.astype(BF16), preferred_element_type=F32))
        g_last = g_row[:, c - 1:c]
        v_dec = (v_new * jnp.exp(g_last - g_col)).astype(BF16)
        state_ref[...] = state * jnp.exp(g_last) + _dot_tn(k, v_dec)
        z = z_ref[0, pl.ds(t0, c), :]
        o = o * lax.rsqrt(jnp.mean(o * o, axis=-1, keepdims=True) + EPS) * gain_ref[...]
        o_ref[0, pl.ds(t0, c), :] = (o * (z * jax.nn.sigmoid(z))).astype(o_ref.dtype)
        return carry

    lax.fori_loop(0, n_chunks, body, 0)


def gdn_core(qkv, proj, gc, beta, o_gain, *, heads):
    b, s, _ = qkv.shape
    n_chunks = s // CHUNK
    dh = GDN_HEAD_DIM
    gc = gc.reshape(b, heads, n_chunks, CHUNK)
    beta = beta.reshape(b, heads, n_chunks, CHUNK)

    def col_spec(off):
        return pl.BlockSpec((1, s, dh), lambda bi, h: (bi, 0, off + h))

    gate_spec = pl.BlockSpec((1, 1, n_chunks, CHUNK), lambda bi, h: (bi, h, 0, 0))
    return pl.pallas_call(
        functools.partial(_gdn_core_kernel, n_chunks=n_chunks),
        out_shape=jax.ShapeDtypeStruct((b, s, heads * dh), BF16),
        grid=(b, heads),
        in_specs=[col_spec(0), col_spec(heads), col_spec(2 * heads), col_spec(3 * heads),
                  gate_spec, gate_spec, pl.BlockSpec((1, dh), lambda bi, h: (0, 0))],
        out_specs=col_spec(0),
        scratch_shapes=[pltpu.VMEM((dh, dh), F32)],
        compiler_params=_params("parallel", "parallel"),
        name="gdn_core",
    )(qkv, qkv, qkv, proj, gc, beta, o_gain.reshape(1, dh).astype(F32))


def gated_deltanet_layer(x2d, b, s, norm_gain, w_in, conv_w, a_log, dt_bias, o_gain, w_out):
    t, d = x2d.shape
    heads = a_log.shape[0]
    width = heads * GDN_HEAD_DIM
    (xn,) = add_rmsnorm([x2d], [norm_gain])
    proj = matmul(xn, w_in[:, :4 * width].astype(BF16), out_dtype=F32)
    bg = matmul(xn, w_in[:, 4 * width:].astype(BF16), out_dtype=F32)
    proj = proj.reshape(b, s, 4 * width)
    qkv = gdn_prep(proj, conv_w.astype(F32), width=width)
    beta, gc = gdn_gates(jnp.swapaxes(bg.reshape(b, s, 2 * heads), 1, 2), a_log, dt_bias)
    o = gdn_core(qkv, proj, gc, beta, o_gain, heads=heads)
    return matmul(o.reshape(t, width), w_out.astype(BF16), out_dtype=F32, residual=x2d)


def _diff_attn_kernel(q_ref, k_ref, v_ref, lq1_ref, lk1_ref, lq2_ref, lk2_ref, gain_ref, o_ref,
                      m_ref, l_ref, acc_ref, *, tq, tk, lambda_init):
    qi = pl.program_id(2)
    d = DIFF_HEAD_DIM
    m_ref[...] = jnp.full_like(m_ref, -jnp.inf)
    l_ref[...] = jnp.zeros_like(l_ref)
    acc_ref[...] = jnp.zeros_like(acc_ref)
    q = q_ref[0]

    def step(j, masked):
        k0 = pl.multiple_of(j * tk, tk)
        kb = k_ref[0, pl.ds(k0, tk), :]
        vb = v_ref[0, pl.ds(k0, tk), :]
        for c in range(2):
            s = _dot_nt(q[:, c * d:(c + 1) * d], kb[:, c * d:(c + 1) * d])
            if masked:
                q_pos = qi * tq + lax.broadcasted_iota(jnp.int32, (tq, tk), 0)
                k_pos = k0 + lax.broadcasted_iota(jnp.int32, (tq, tk), 1)
                s = jnp.where(k_pos <= q_pos, s, -jnp.inf)
            m_prev = m_ref[c]
            m_new = jnp.maximum(m_prev, jnp.max(s, axis=-1, keepdims=True))
            alpha = jnp.exp(m_prev - m_new)
            p = jnp.exp(s - m_new)
            l_ref[c] = alpha * l_ref[c] + jnp.sum(p, axis=-1, keepdims=True)
            acc_ref[c] = alpha * acc_ref[c] + jnp.dot(p.astype(BF16), vb, preferred_element_type=F32)
            m_ref[c] = m_new

    n_full = (qi * tq) // tk
    n_all = ((qi + 1) * tq + tk - 1) // tk

    def full_body(j, carry):
        step(j, False)
        return carry

    def diag_body(j, carry):
        step(j, True)
        return carry

    lax.fori_loop(0, n_full, full_body, 0)
    lax.fori_loop(n_full, n_all, diag_body, 0)

    lam = (jnp.exp(jnp.sum(lq1_ref[...] * lk1_ref[...], axis=-1, keepdims=True))
           - jnp.exp(jnp.sum(lq2_ref[...] * lk2_ref[...], axis=-1, keepdims=True)) + lambda_init)
    o = acc_ref[0] / l_ref[0] - lam * (acc_ref[1] / l_ref[1])
    o = o * lax.rsqrt(jnp.mean(o * o, axis=-1, keepdims=True) + EPS)
    o_ref[0] = (o * gain_ref[...] * (1.0 - lambda_init)).astype(o_ref.dtype)


def diff_attention(q, k, v, lam_q1, lam_k1, lam_q2, lam_k2, sub_gain, lambda_init, *, tq=512, tk=512):
    b, s, width = q.shape
    tq, tk = min(tq, s), min(tk, s)
    hd = 2 * DIFF_HEAD_DIM
    heads = width // hd
    vec = lambda a: a.reshape(1, -1).astype(F32)
    vec_spec = pl.BlockSpec((1, DIFF_HEAD_DIM), lambda bi, h, i: (0, 0))
    kv_spec = pl.BlockSpec((1, s, hd), lambda bi, h, i: (bi, 0, h))
    q_spec = pl.BlockSpec((1, tq, hd), lambda bi, h, i: (bi, i, h))
    return pl.pallas_call(
        functools.partial(_diff_attn_kernel, tq=tq, tk=tk, lambda_init=lambda_init),
        out_shape=jax.ShapeDtypeStruct((b, s, width), BF16),
        grid=(b, heads, s // tq),
        in_specs=[q_spec, kv_spec, kv_spec, vec_spec, vec_spec, vec_spec, vec_spec,
                  pl.BlockSpec((1, hd), lambda bi, h, i: (0, 0))],
        out_specs=q_spec,
        scratch_shapes=[pltpu.VMEM((2, tq, 1), F32), pltpu.VMEM((2, tq, 1), F32), pltpu.VMEM((2, tq, hd), F32)],
        compiler_params=_params("parallel", "parallel", "parallel"),
        name="diff_attention",
    )(q, k, v, vec(lam_q1), vec(lam_k1), vec(lam_q2), vec(lam_k2), vec(sub_gain))


def diff_attention_layer(x2d, b, s, kv_gain, kv_w, k_gain, norm_gain, w_q, q_gain,
                         lam_q1, lam_k1, lam_q2, lam_k2, sub_gain, w_o, lambda_init, extra=None):
    t, d = x2d.shape
    if extra is None:
        xkv, xb = add_rmsnorm([x2d], [kv_gain, norm_gain])
    else:
        x2d, xkv, xb = add_rmsnorm([x2d, extra], [kv_gain, norm_gain])
    qk_width = w_q.shape[1]
    scale = DIFF_HEAD_DIM ** -0.5
    k = matmul(xkv, kv_w[:, :qk_width].astype(BF16), out_dtype=BF16, group_gain=k_gain)
    v = matmul(xkv, kv_w[:, qk_width:].astype(BF16), out_dtype=BF16)
    q = matmul(xb, w_q.astype(BF16), out_dtype=BF16, group_gain=q_gain * scale)
    o = diff_attention(q.reshape(b, s, -1), k.reshape(b, s, -1), v.reshape(b, s, -1),
                       lam_q1, lam_k1, lam_q2, lam_k2, sub_gain, lambda_init)
    return matmul(o.reshape(t, -1), w_o.astype(BF16), out_dtype=F32, residual=x2d)


def _take_top(work, k):
    n = work.shape[0]
    idx = lax.broadcasted_iota(jnp.int32, work.shape, 0)
    vals = []
    for _ in range(k):
        mx = jnp.max(work, axis=0, keepdims=True)
        vals.append(mx)
        first = jnp.min(jnp.where(work == mx, idx, n), axis=0, keepdims=True)
        work = jnp.where(idx == first, -jnp.inf, work)
    return vals


def _peer_route_kernel(qry_ref, k1_ref, k2_ref, s1_ref, s2_ref, tau_ref, mz_ref):
    taus, mzs = [], []
    for h in range(PEER_HEADS):
        tops = []
        for side, (k_ref, s_ref) in enumerate(((k1_ref, s1_ref), (k2_ref, s2_ref))):
            c0 = (2 * h + side) * PEER_HALF
            qh = qry_ref[:, c0:c0 + PEER_HALF].astype(BF16)
            st = _dot_nt(k_ref[h].astype(BF16), qh)
            s_ref[h] = st
            tops.append(_take_top(st, PEER_TOPK))
        v1 = jnp.concatenate(tops[0], axis=0)
        cand = jnp.concatenate([v1 + tops[1][bb] for bb in range(PEER_TOPK)], axis=0)
        best = _take_top(cand, PEER_TOPK)
        m = best[0]
        z = jnp.zeros_like(m)
        for val in best:
            z = z + jnp.exp(val - m)
        taus.append(best[-1])
        mzs.append(m + jnp.log(z))
    tau_ref[...] = jnp.concatenate(taus, axis=0)
    mz_ref[...] = jnp.concatenate(mzs, axis=0)


def peer_route(qry, keys1, keys2, *, tm=256):
    t = qry.shape[0]
    sc_shape = jax.ShapeDtypeStruct((PEER_HEADS, PEER_KEYS, t), F32)
    th_shape = jax.ShapeDtypeStruct((PEER_HEADS, t), F32)
    key_spec = pl.BlockSpec((PEER_HEADS, PEER_KEYS, PEER_HALF), lambda i: (0, 0, 0))
    sc_spec = pl.BlockSpec((PEER_HEADS, PEER_KEYS, tm), lambda i: (0, 0, i))
    th_spec = pl.BlockSpec((PEER_HEADS, tm), lambda i: (0, i))
    return pl.pallas_call(
        _peer_route_kernel,
        out_shape=[sc_shape, sc_shape, th_shape, th_shape],
        grid=(t // tm,),
        in_specs=[pl.BlockSpec((tm, qry.shape[1]), lambda i: (i, 0)), key_spec, key_spec],
        out_specs=[sc_spec, sc_spec, th_spec, th_spec],
        compiler_params=_params("parallel"),
        name="peer_route",
    )(qry, keys1.astype(F32), keys2.astype(F32))


def _peer_expert_kernel(x_ref, u_ref, v_ref, s1_ref, s2_ref, tau_ref, mz_ref, o_ref, *, te):
    j = pl.program_id(1)

    @pl.when(j == 0)
    def _():
        o_ref[...] = jnp.zeros_like(o_ref)

    act = _dot_nt(u_ref[...], x_ref[...])
    act = 0.5 * act * (1.0 + lax.erf(act * (2.0 ** -0.5)))
    rows_per_tile = te // PEER_KEYS
    parts = []
    for r in range(rows_per_tile):
        i1 = j * rows_per_tile + r
        gate = None
        for h in range(PEER_HEADS):
            s = s1_ref[h, pl.ds(i1, 1), :] + s2_ref[h]
            g = jnp.where(s >= tau_ref[h:h + 1, :], jnp.exp(s - mz_ref[h:h + 1, :]), 0.0)
            gate = g if gate is None else gate + g
        parts.append(gate * act[r * PEER_KEYS:(r + 1) * PEER_KEYS, :])
    p = jnp.concatenate(parts, axis=0) if len(parts) > 1 else parts[0]
    o_ref[...] += _dot_tn(p.astype(BF16), v_ref[...])


def peer_experts(xn, u, v, s1, s2, tau, mz, *, tm=512, te=256):
    t, d = xn.shape
    tm = min(tm, t)
    e = u.shape[0]
    w_spec = pl.BlockSpec((te, d), lambda i, j: (j, 0))
    sc_spec = pl.BlockSpec((PEER_HEADS, PEER_KEYS, tm), lambda i, j: (0, 0, i))
    th_spec = pl.BlockSpec((PEER_HEADS, tm), lambda i, j: (0, i))
    return pl.pallas_call(
        functools.partial(_peer_expert_kernel, te=te),
        out_shape=jax.ShapeDtypeStruct((t, d), F32),
        grid=(t // tm, e // te),
        in_specs=[pl.BlockSpec((tm, d), lambda i, j: (i, 0)), w_spec, w_spec,
                  sc_spec, sc_spec, th_spec, th_spec],
        out_specs=pl.BlockSpec((tm, d), lambda i, j: (i, 0)),
        compiler_params=_params("parallel", "arbitrary"),
        name="peer_experts",
    )(xn, u, v, s1, s2, tau, mz)


def peer_layer(x2d, norm_gain, w_query, keys1, keys2, expert_u, expert_v):
    (xn,) = add_rmsnorm([x2d], [norm_gain])
    qry = matmul(xn, w_query.astype(BF16), out_dtype=F32)
    s1, s2, tau, mz = peer_route(qry, keys1, keys2)
    return peer_experts(xn, expert_u.astype(BF16), expert_v.astype(BF16), s1, s2, tau, mz)


def kernel(x, a_norm, a_w_in, a_conv, a_log, a_dt_bias, a_out_norm, a_w_out, kv_norm, kv_w, kv_k_norm, b_norm, b_w_q, b_q_norm, b_lam_q1, b_lam_k1, b_lam_q2, b_lam_k2, b_sub_norm, b_w_o, f_norm, f_w_query, f_sub_keys1, f_sub_keys2, f_expert_u, f_expert_v):
    b, s, d = x.shape
    n_a = a_norm.shape[0]
    depth = f_norm.shape[0]
    h = x.reshape(b * s, d)
    pending = None
    for i in range(depth):
        if i < n_a:
            if pending is not None:
                h = add(h, pending)
            h = gated_deltanet_layer(h, b, s, a_norm[i], a_w_in[i], a_conv[i], a_log[i], a_dt_bias[i],
                                     a_out_norm[i], a_w_out[i])
        else:
            j = i - n_a
            lambda_init = 0.8 - 0.6 * math.exp(-0.3 * i)
            if j > 0:
                raise NotImplementedError("shared K/V reuse across several attention layers")
            h = diff_attention_layer(h, b, s, kv_norm, kv_w, kv_k_norm, b_norm[j], b_w_q[j], b_q_norm[j],
                                     b_lam_q1[j], b_lam_k1[j], b_lam_q2[j], b_lam_k2[j], b_sub_norm[j],
                                     b_w_o[j], lambda_init, extra=pending)
        pending = peer_layer(h, f_norm[i], f_w_query[i], f_sub_keys1[i], f_sub_keys2[i],
                             f_expert_u[i], f_expert_v[i])
    return add(h, pending).reshape(b, s, d)
```
